```python
import jax
import jax.numpy as jnp
from jax import lax
import numpy as np

D_MODEL = 1024
BATCH = 4
SEQ = 8192
DEPTH = 1
DEC_BATCH = 128
DEC_SEQ = 8
PAST_LEN = 16384
PAGE_SIZE = 128

MLA_HEADS = 8
MLA_NOPE = 64
MLA_ROPE = 32
MLA_VDIM = 64
MLA_Q_RANK = 384
MLA_KV_RANK = 256
ROPE_THETA = 10000.0
MOBA_HEADS = 8
MOBA_HDIM = 64
MOBA_BLOCK = 256
MOBA_TOPK = 3
MEM_TOKENS = 256
MEM_HEADS = 4
MEM_HDIM = 128
MLA_W = MLA_HEADS * MLA_VDIM
MOBA_W = MOBA_HEADS * MOBA_HDIM
MEM_W = MEM_HEADS * MEM_HDIM
BRANCH_W = 512
N_BRANCH = 3
D_FF = 4 * D_MODEL
EPS = 1e-6
NEG_INF = -1e30
Q_BLOCK = 128
MOBA_Q_CHUNK = 32
MLA_SCALE = (MLA_NOPE + MLA_ROPE) ** -0.5
MOBA_SCALE = MOBA_HDIM ** -0.5
MEM_SCALE = MEM_HDIM ** -0.5
IN_WIDTHS = (MLA_Q_RANK, MLA_KV_RANK, MLA_ROPE, MOBA_W, MOBA_W, MOBA_W, MEM_W, N_BRANCH * D_MODEL)

kernel_name = 'hybrid_mla_moba_memory_gated_decoder_step'


def _split_points():
    pts, acc = [], 0
    for w in IN_WIDTHS[:-1]:
        acc += w
        pts.append(acc)
    return pts


def _rmsnorm(x, g):
    xf = x.astype(jnp.float32)
    xf = xf * lax.rsqrt(jnp.mean(xf * xf, axis=-1, keepdims=True) + EPS)
    return xf.astype(x.dtype) * g


def _rope(x, pos):
    d = x.shape[-1]
    inv = ROPE_THETA ** (-jnp.arange(0, d, 2, dtype=jnp.float32) / d)
    ang = pos.astype(jnp.float32)[:, None] * inv[None, :]
    ang = ang.reshape(ang.shape[:1] + (1,) * (x.ndim - 3) + ang.shape[1:])
    cos, sin = jnp.cos(ang), jnp.sin(ang)
    xf = x.astype(jnp.float32)
    x1, x2 = xf[..., : d // 2], xf[..., d // 2:]
    return jnp.concatenate([x1 * cos - x2 * sin, x1 * sin + x2 * cos], axis=-1).astype(x.dtype)


def _alibi_slopes(n_heads):
    return 2.0 ** (-8.0 * jnp.arange(1, n_heads + 1, dtype=jnp.float32) / n_heads)


def _softmax_parts(*scores):
    p = jax.nn.softmax(jnp.concatenate(scores, axis=-1), axis=-1)
    cuts, acc = [], 0
    for sc in scores[:-1]:
        acc += sc.shape[-1]
        cuts.append(acc)
    return jnp.split(p, cuts, axis=-1)


def _to_blocks(x, size):
    n, t = x.shape[:2]
    return jnp.moveaxis(x.reshape((n, t // size, size) + x.shape[2:]), 1, 0)


def _from_blocks(y):
    nb, n, size = y.shape[:3]
    return jnp.moveaxis(y, 0, 1).reshape((n, nb * size) + y.shape[3:])


def _branch_inputs(h, pos, w_in, b_gate, g_q_a, w_uq, g_kv_a, w_ukv):
    n, t, _ = h.shape
    c_q, c_kv, k_rope, q_mb, k_mb, v_mb, q_mem, g_logit = jnp.split(h @ w_in, _split_points(), axis=-1)
    q = (_rmsnorm(c_q, g_q_a) @ w_uq).reshape(n, t, MLA_HEADS, MLA_NOPE + MLA_ROPE)
    q_rope = _rope(q[..., MLA_NOPE:], pos)
    q_lat = jnp.einsum('nthd,rhd->nthr', q[..., :MLA_NOPE], w_ukv[..., :MLA_NOPE])
    c_kv = _rmsnorm(c_kv, g_kv_a)
    k_rope = _rope(k_rope, pos)
    heads = lambda a: a.reshape(n, t, MOBA_HEADS, MOBA_HDIM)
    gates = jax.nn.sigmoid((g_logit.reshape(n, t, N_BRANCH, D_MODEL) + b_gate).astype(jnp.float32)).astype(h.dtype)
    return (q_lat, q_rope, c_kv, k_rope, heads(q_mb), heads(k_mb), heads(v_mb),
            q_mem.reshape(n, t, MEM_HEADS, MEM_HDIM), gates)


def _mla_scores(q_lat, q_rope, ckv, krope):
    s = jnp.einsum('nqhr,nkr->nhqk', q_lat, ckv).astype(jnp.float32)
    s = s + jnp.einsum('nqhd,nkd->nhqk', q_rope, krope).astype(jnp.float32)
    return s * MLA_SCALE


def _mla_out(out_lat, w_ukv):
    n, t = out_lat.shape[:2]
    return jnp.einsum('nthr,rhd->nthd', out_lat, w_ukv[..., MLA_NOPE:]).reshape(n, t, MLA_W)


def _mla_prompt(q_lat, q_rope, ckv, krope):
    t = q_lat.shape[1]
    k_pos = jnp.arange(t, dtype=jnp.int32)

    def body(args):
        i, ql, qr = args
        t_pos = i * Q_BLOCK + jnp.arange(Q_BLOCK, dtype=jnp.int32)
        s = jnp.where(k_pos[None, :] <= t_pos[:, None], _mla_scores(ql, qr, ckv, krope), NEG_INF)
        p = jax.nn.softmax(s, axis=-1).astype(ckv.dtype)
        return jnp.einsum('nhqk,nkr->nqhr', p, ckv)

    xs = (jnp.arange(t // Q_BLOCK, dtype=jnp.int32), _to_blocks(q_lat, Q_BLOCK), _to_blocks(q_rope, Q_BLOCK))
    return _from_blocks(lax.map(body, xs))


def _mla_sample(q_lat, q_rope, ckv_new, krope_new, cache_ckv, cache_krope, page_table):
    n, s_len = q_lat.shape[:2]
    past_len = page_table.shape[1] * cache_ckv.shape[1]
    ckv_past = cache_ckv[page_table].reshape(n, past_len, MLA_KV_RANK)
    krope_past = cache_krope[page_table].reshape(n, past_len, MLA_ROPE)
    s_past = _mla_scores(q_lat, q_rope, ckv_past, krope_past)
    ar = jnp.arange(s_len)
    s_new = jnp.where(ar[None, :] <= ar[:, None], _mla_scores(q_lat, q_rope, ckv_new, krope_new), NEG_INF)
    p_past, p_new = _softmax_parts(s_past, s_new)
    return (jnp.einsum('nhqk,nkr->nqhr', p_past.astype(ckv_past.dtype), ckv_past)
            + jnp.einsum('nhqk,nkr->nqhr', p_new.astype(ckv_new.dtype), ckv_new))


def _moba_core(q, t_pos, sel, k_loc, v_loc, loc_pos, slopes):
    m = slopes[:, None, None]
    s_loc = (jnp.einsum('nqhd,nshd->nhqs', q, k_loc).astype(jnp.float32) * MOBA_SCALE
             - m * (t_pos[:, None] - loc_pos[None, :]).astype(jnp.float32))
    loc_ok = (loc_pos[None, :] <= t_pos[:, None]) & (loc_pos[None, :] // MOBA_BLOCK == t_pos[:, None] // MOBA_BLOCK)
    s_loc = jnp.where(loc_ok, s_loc, NEG_INF)
    if sel is None:
        p_loc = jax.nn.softmax(s_loc, axis=-1)
        return jnp.einsum('nhqs,nshd->nqhd', p_loc.astype(v_loc.dtype), v_loc)
    ksel, vsel, sel_pos, sel_valid = sel
    s_sel = (jnp.einsum('nqhd,nhqld->nhql', q, ksel).astype(jnp.float32) * MOBA_SCALE
             - m * (t_pos[:, None] - sel_pos).astype(jnp.float32))
    if sel_valid is not None:
        s_sel = jnp.where(sel_valid, s_sel, NEG_INF)
    p_sel, p_loc = _softmax_parts(s_sel, s_loc)
    return (jnp.einsum('nhql,nhqld->nqhd', p_sel.astype(vsel.dtype), vsel)
            + jnp.einsum('nhqs,nshd->nqhd', p_loc.astype(v_loc.dtype), v_loc))


def _moba_prompt(q, k, v):
    n, t, h, d = q.shape
    nb = -(-t // MOBA_BLOCK)
    pad = ((0, 0), (0, nb * MOBA_BLOCK - t), (0, 0), (0, 0))
    k_blk = jnp.pad(k, pad).reshape(n, nb, MOBA_BLOCK, h, d)
    v_blk = jnp.pad(v, pad).reshape(n, nb, MOBA_BLOCK, h, d)
    slopes = _alibi_slopes(h)
    topk = min(MOBA_TOPK, nb - 1)
    pos = jnp.arange(t, dtype=jnp.int32)
    xs = (jnp.arange(t // MOBA_Q_CHUNK, dtype=jnp.int32), _to_blocks(q, MOBA_Q_CHUNK))
    if topk > 0:
        k_mean = jnp.mean(k_blk.astype(jnp.float32), axis=2)
        gate = jnp.einsum('nthd,nbhd->nhtb', q.astype(jnp.float32), k_mean)
        past = jnp.arange(nb)[None, :] < (pos // MOBA_BLOCK)[:, None]
        _, idx = lax.top_k(jnp.where(past, gate, NEG_INF), topk)
        xs = xs + (_to_blocks(jnp.moveaxis(idx, 2, 1), MOBA_Q_CHUNK),)
    n_ix = jnp.arange(n)[:, None, None, None]
    h_ix = jnp.arange(h)[None, :, None, None]

    def body(args):
        i, q_c = args[0], args[1]
        t_pos = i * MOBA_Q_CHUNK + jnp.arange(MOBA_Q_CHUNK, dtype=jnp.int32)
        c = (i * MOBA_Q_CHUNK) // MOBA_BLOCK
        k_loc = lax.dynamic_index_in_dim(k_blk, c, axis=1, keepdims=False)
        v_loc = lax.dynamic_index_in_dim(v_blk, c, axis=1, keepdims=False)
        loc_pos = c * MOBA_BLOCK + jnp.arange(MOBA_BLOCK, dtype=jnp.int32)
        sel = None
        if topk > 0:
            idx_c = jnp.swapaxes(args[2], 1, 2)
            length = topk * MOBA_BLOCK
            ksel = k_blk[n_ix, idx_c, :, h_ix].reshape(n, h, MOBA_Q_CHUNK, length, d)
            vsel = v_blk[n_ix, idx_c, :, h_ix].reshape(n, h, MOBA_Q_CHUNK, length, d)
            sel_pos = (idx_c[..., None] * MOBA_BLOCK + jnp.arange(MOBA_BLOCK)).reshape(n, h, MOBA_Q_CHUNK, length)
            sel_valid = jnp.repeat(jnp.arange(topk) < c, MOBA_BLOCK)
            sel = (ksel, vsel, sel_pos, sel_valid)
        return _moba_core(q_c, t_pos, sel, k_loc, v_loc, loc_pos, slopes)

    return _from_blocks(lax.map(body, xs))


def _moba_sample(q, k_new, v_new, t_pos, cache_k, cache_v, page_table):
    n, s_len, h, d = q.shape
    page = cache_k.shape[1]
    past_len = page_table.shape[1] * page
    ppb = MOBA_BLOCK // page
    n_full = past_len // MOBA_BLOCK
    own_start = n_full * MOBA_BLOCK
    slopes = _alibi_slopes(h)
    own_pages = page_table[:, n_full * ppb:]
    k_loc = jnp.concatenate([cache_k[own_pages].reshape(n, past_len - own_start, h, d), k_new], axis=1)
    v_loc = jnp.concatenate([cache_v[own_pages].reshape(n, past_len - own_start, h, d), v_new], axis=1)
    loc_pos = jnp.concatenate([jnp.arange(own_start, past_len, dtype=jnp.int32), t_pos])
    topk = min(MOBA_TOPK, n_full)
    if topk == 0:
        return _moba_core(q, t_pos, None, k_loc, v_loc, loc_pos, slopes)
    k_past = cache_k[page_table[:, : n_full * ppb]].astype(jnp.float32)
    k_mean = jnp.mean(k_past.reshape(n, n_full, MOBA_BLOCK, h, d), axis=2)
    gate = jnp.einsum('nqhd,nbhd->nhqb', q.astype(jnp.float32), k_mean)
    _, idx = lax.top_k(gate, topk)
    n_ix = jnp.arange(n)[:, None, None, None, None]
    h_ix = jnp.arange(h)[None, :, None, None, None]
    length = topk * MOBA_BLOCK

    def body(args):
        q_c, idx_c, tp = args
        idx_c = jnp.swapaxes(idx_c, 1, 2)
        phys = page_table[n_ix, idx_c[..., None] * ppb + jnp.arange(ppb)]
        ksel = cache_k[phys, :, h_ix].reshape(n, h, 1, length, d)
        vsel = cache_v[phys, :, h_ix].reshape(n, h, 1, length, d)
        sel_pos = (idx_c[..., None] * MOBA_BLOCK + jnp.arange(MOBA_BLOCK)).reshape(n, h, 1, length)
        return _moba_core(q_c, tp, (ksel, vsel, sel_pos, None), k_loc, v_loc, loc_pos, slopes)

    xs = (_to_blocks(q, 1), _to_blocks(jnp.moveaxis(idx, 2, 1), 1), t_pos.reshape(s_len, 1))
    return _from_blocks(lax.map(body, xs))


def _mem_kv(mem, g_mem, w_mem_kv):
    n, m, _ = mem.shape
    k, v = jnp.split(_rmsnorm(mem, g_mem) @ w_mem_kv, 2, axis=-1)
    return k.reshape(n, m, MEM_HEADS, MEM_HDIM), v.reshape(n, m, MEM_HEADS, MEM_HDIM)


def _mem_attend(q_mem, mem_k, mem_v):
    n, t = q_mem.shape[:2]
    s = jnp.einsum('nthd,nmhd->nhtm', q_mem, mem_k).astype(jnp.float32) * MEM_SCALE
    p = jax.nn.softmax(s, axis=-1).astype(mem_v.dtype)
    return jnp.einsum('nhtm,nmhd->nthd', p, mem_v).reshape(n, t, MEM_W)


def _merge_and_finish(x, o_mla, o_moba, o_mem, gates, w_branch, w_o, g_mlp, w_up, w_down, g_final):
    o = jnp.stack([o_mla, o_moba, o_mem], axis=2)
    up = jnp.einsum('ntbk,bkd->ntbd', o, w_branch)
    x = x + jnp.sum(gates * up, axis=2) @ w_o
    hm = _rmsnorm(x, g_mlp)
    x = x + jnp.square(jax.nn.relu(hm @ w_up)) @ w_down
    return _rmsnorm(x, g_final)


def setup_inputs(seed: int = 0) -> dict:
    key = jax.random.key(seed)
    ks = jax.random.split(key, 26)
    f32 = jnp.float32
    n_pages = PAST_LEN // PAGE_SIZE
    n_used = DEC_BATCH * n_pages
    n_pool = n_used + n_used // 4
    d_in = sum(IN_WIDTHS)

    def nrm(k, shape, scale=1.0):
        return scale * jax.random.normal(k, shape, f32)

    def gain(k, n):
        return 1.0 + 0.1 * jax.random.normal(k, (n,), f32)

    page_table = jax.random.permutation(ks[9], n_pool)[:n_used].reshape(DEC_BATCH, n_pages).astype(jnp.int32)
    return {
        'x_prompt': nrm(ks[0], (BATCH, SEQ, D_MODEL)),
        'x_sample': nrm(ks[1], (DEC_BATCH, DEC_SEQ, D_MODEL)),
        'mem_prompt': nrm(ks[2], (BATCH, MEM_TOKENS, D_MODEL)),
        'cache_mla_ckv': nrm(ks[3], (n_pool, PAGE_SIZE, MLA_KV_RANK)),
        'cache_mla_krope': nrm(ks[4], (n_pool, PAGE_SIZE, MLA_ROPE)),
        'cache_moba_k': nrm(ks[5], (n_pool, PAGE_SIZE, MOBA_HEADS, MOBA_HDIM)),
        'cache_moba_v': nrm(ks[6], (n_pool, PAGE_SIZE, MOBA_HEADS, MOBA_HDIM)),
        'cache_mem_k': nrm(ks[7], (DEC_BATCH, MEM_TOKENS, MEM_HEADS, MEM_HDIM)),
        'cache_mem_v': nrm(ks[8], (DEC_BATCH, MEM_TOKENS, MEM_HEADS, MEM_HDIM)),
        'page_table': page_table,
        'g_attn': gain(ks[10], D_MODEL),
        'w_in': nrm(ks[11], (D_MODEL, d_in), D_MODEL ** -0.5),
        'b_gate': nrm(ks[12], (N_BRANCH, D_MODEL), 0.1),
        'g_q_a': gain(ks[13], MLA_Q_RANK),
        'w_uq': nrm(ks[14], (MLA_Q_RANK, MLA_HEADS * (MLA_NOPE + MLA_ROPE)), MLA_Q_RANK ** -0.5),
        'g_kv_a': gain(ks[15], MLA_KV_RANK),
        'w_ukv': nrm(ks[16], (MLA_KV_RANK, MLA_HEADS, MLA_NOPE + MLA_VDIM), MLA_KV_RANK ** -0.5),
        'g_mem': gain(ks[17], D_MODEL),
        'w_mem_kv': nrm(ks[18], (D_MODEL, 2 * MEM_W), D_MODEL ** -0.5),
        'w_branch': nrm(ks[19], (N_BRANCH, BRANCH_W, D_MODEL), BRANCH_W ** -0.5),
        'w_o': nrm(ks[20], (D_MODEL, D_MODEL), D_MODEL ** -0.5),
        'g_mlp': gain(ks[21], D_MODEL),
        'w_up': nrm(ks[22], (D_MODEL, D_FF), D_MODEL ** -0.5),
        'w_down': nrm(ks[23], (D_FF, D_MODEL), D_FF ** -0.5),
        'g_final': gain(ks[24], D_MODEL),
    }


def reference(x_prompt, x_sample, mem_prompt, cache_mla_ckv, cache_mla_krope, cache_moba_k, cache_moba_v,
              cache_mem_k, cache_mem_v, page_table, g_attn, w_in, b_gate, g_q_a, w_uq, g_kv_a, w_ukv,
              g_mem, w_mem_kv, w_branch, w_o, g_mlp, w_up, w_down, g_final):
    b, t, _ = x_prompt.shape
    db, s_len, _ = x_sample.shape
    past_len = page_table.shape[1] * cache_moba_k.shape[1]
    proj = (w_in, b_gate, g_q_a, w_uq, g_kv_a, w_ukv)
    finish = (w_branch, w_o, g_mlp, w_up, w_down, g_final)

    pos_p = jnp.arange(t, dtype=jnp.int32)
    (q_lat, q_rope, ckv_p, krope_p, q_mb, k_mb_p, v_mb_p, q_mem, gates) = _branch_inputs(
        _rmsnorm(x_prompt, g_attn), pos_p, *proj)
    o_mla = _mla_out(_mla_prompt(q_lat, q_rope, ckv_p, krope_p), w_ukv)
    o_moba = _moba_prompt(q_mb, k_mb_p, v_mb_p).reshape(b, t, MOBA_W)
    mem_k_p, mem_v_p = _mem_kv(mem_prompt, g_mem, w_mem_kv)
    o_mem = _mem_attend(q_mem, mem_k_p, mem_v_p)
    y_prompt = _merge_and_finish(x_prompt, o_mla, o_moba, o_mem, gates, *finish)

    pos_s = past_len + jnp.arange(s_len, dtype=jnp.int32)
    (q_lat_s, q_rope_s, ckv_s, krope_s, q_mb_s, k_mb_s, v_mb_s, q_mem_s, gates_s) = _branch_inputs(
        _rmsnorm(x_sample, g_attn), pos_s, *proj)
    o_mla_s = _mla_out(_mla_sample(q_lat_s, q_rope_s, ckv_s, krope_s, cache_mla_ckv, cache_mla_krope, page_table), w_ukv)
    o_moba_s = _moba_sample(q_mb_s, k_mb_s, v_mb_s, pos_s, cache_moba_k, cache_moba_v, page_table).reshape(db, s_len, MOBA_W)
    o_mem_s = _mem_attend(q_mem_s, cache_mem_k, cache_mem_v)
    y_sample = _merge_and_finish(x_sample, o_mla_s, o_moba_s, o_mem_s, gates_s, *finish)

    return (y_prompt, y_sample, ckv_p, krope_p, k_mb_p, v_mb_p, mem_k_p, mem_v_p, ckv_s, krope_s, k_mb_s, v_mb_s)
```

```python
import functools
import math

import numpy as np
import jax
import jax.numpy as jnp
from jax import lax
from jax.experimental import pallas as pl
from jax.experimental.pallas import tpu as pltpu

F32 = jnp.float32
BF16 = jnp.bfloat16

D_MODEL = 1024
MLA_HEADS = 8
MLA_NOPE = 64
MLA_ROPE = 32
MLA_VDIM = 64
MLA_Q_RANK = 384
MLA_KV_RANK = 256
ROPE_THETA = 10000.0
MOBA_HEADS = 8
MOBA_HDIM = 64
MOBA_BLOCK = 256
MOBA_TOPK = 3
MEM_TOKENS = 256
MEM_HEADS = 4
MEM_HDIM = 128
MOBA_W = MOBA_HEADS * MOBA_HDIM
MEM_W = MEM_HEADS * MEM_HDIM
BRANCH_W = 512
N_BRANCH = 3
D_FF = 4 * D_MODEL
EPS = 1e-6
NEG = -1e30
LOG2E = math.log2(math.e)
MLA_QSCALE = (MLA_NOPE + MLA_ROPE) ** -0.5 * LOG2E
MOBA_QSCALE = MOBA_HDIM ** -0.5 * LOG2E
MEM_QSCALE = MEM_HDIM ** -0.5 * LOG2E

LANES = 128
HEAD_PAD = 128
VMEM_CAP = 56 * 1024 * 1024


def _alibi_slopes_log2():
    return [2.0 ** (-8.0 * (h + 1) / MOBA_HEADS) * LOG2E for h in range(MOBA_HEADS)]


def _split3_bf16(v):
    out = []
    r = np.float32(v)
    for _ in range(3):
        p = np.float32(np.asarray(r, np.float32).astype(BF16))
        out.append(float(p))
        r = np.float32(r - p)
    return out


def _params(sem, est_bytes):
    limit = int(min(VMEM_CAP, est_bytes * 1.25 + (6 << 20)))
    return pltpu.CompilerParams(dimension_semantics=sem, vmem_limit_bytes=limit)


def _rms(x):
    return x * lax.rsqrt(jnp.mean(x * x, axis=-1, keepdims=True) + EPS)


def _dot(a, b):
    return jnp.dot(a, b, preferred_element_type=F32)


def _dot_nt(a, b, precision=None):
    return lax.dot_general(a, b, (((1,), (1,)), ((), ())), preferred_element_type=F32, precision=precision)


def _full(shape):
    nd = len(shape)
    return pl.BlockSpec(shape, lambda *_: (0,) * nd)


def _mem_kv_kernel(mem_ref, g_ref, w_ref, k_ref, v_ref):
    hb = (_rms(mem_ref[0]) * g_ref[...]).astype(BF16)
    k_ref[0] = _dot(hb, w_ref[:, :MEM_W])
    v_ref[0] = _dot(hb, w_ref[:, MEM_W:])


def _mem_kv(mem, g_mem, w_mem_kv):
    g, m, _ = mem.shape
    out = jax.ShapeDtypeStruct((g, m, MEM_W), F32)
    return pl.pallas_call(
        _mem_kv_kernel,
        grid=(g,),
        in_specs=[pl.BlockSpec((1, m, D_MODEL), lambda i: (i, 0, 0)), _full((1, D_MODEL)),
                  _full((D_MODEL, 2 * MEM_W))],
        out_specs=[pl.BlockSpec((1, m, MEM_W), lambda i: (i, 0, 0))] * 2,
        out_shape=[out, out],
        compiler_params=_params(("arbitrary",), 16 << 20),
        name="mem_kv",
    )(mem, g_mem.reshape(1, -1), w_mem_kv.astype(BF16))


_C_CQ = (0, 384)
_C_CKV = (384, 640)
_C_QMB = (640, 1152)
_C_KMB = (1152, 1664)
_C_VMB = (1664, 2176)
_C_QMEM = (2176, 2688)
_C_KR = (2688, 2816)
_WA_COLS = 2816


def _rope_table():
    half = MLA_ROPE // 2
    inv = ROPE_THETA ** (-jnp.arange(0, MLA_ROPE, 2, dtype=F32) / MLA_ROPE)
    gap = jnp.zeros((LANES // 2 - MLA_ROPE,), F32)
    inv_row = jnp.concatenate([inv, inv, gap, inv, inv, gap])
    t = np.zeros((8, LANES), np.float32)
    for i in range(MLA_ROPE):
        t[1, MLA_NOPE + i] = 1.0
        t[3, MLA_NOPE + i] = -1.0 if i < half else 1.0
        t[4, i] = 1.0
        t[5, i] = -1.0 if i < half else 1.0
    t[2, :MLA_NOPE] = 1.0
    return jnp.concatenate([inv_row[None, :], jnp.asarray(t[1:])], axis=0)


def _in_proj_kernel(x_ref, gattn_ref, wa_ref, gqa_ref, wuq_ref, gkv_ref, wuk_ref, tab_ref,
                    ckv_ref, ckvb_ref, krope_ref, kmb_ref, vmb_ref, qmb_ref, qmem_ref, qcat_ref, *rest,
                    tm, pos_base, period, with_kcat):
    i = pl.program_id(0)
    hb = (_rms(x_ref[...]) * gattn_ref[...]).astype(BF16)

    def proj(c):
        return _dot(hb, wa_ref[:, c[0]:c[1]])

    row = i * tm + lax.broadcasted_iota(jnp.int32, (tm, 1), 0)
    pos = (pos_base + (row & (period - 1))).astype(F32)
    tab = tab_ref[...]
    ang = pos * tab[0:1]
    cos_all = jnp.cos(ang)
    sin_all = jnp.sin(ang)
    cosq = (cos_all * tab[1:2] + tab[2:3]) * MLA_QSCALE
    sinq = sin_all * tab[3:4] * MLA_QSCALE
    cosk = cos_all * tab[4:5]
    sink = sin_all * tab[5:6]

    cqn = (_rms(proj(_C_CQ)) * gqa_ref[...]).astype(BF16)
    zq = _dot(cqn, wuq_ref[...])
    for h in range(MLA_HEADS):
        z = zq[:, h * HEAD_PAD:(h + 1) * HEAD_PAD]
        qc = z * cosq + pltpu.roll(z, HEAD_PAD - MLA_ROPE, 1) * sinq
        qcat_ref[:, h * HEAD_PAD:(h + 1) * HEAD_PAD] = qc.astype(BF16)

    kr = proj(_C_KR)
    roped = kr * cosk + pltpu.roll(kr, LANES - MLA_ROPE, 1) * sink
    krope_ref[...] = roped[:, :MLA_ROPE]

    ckv = _rms(proj(_C_CKV)) * gkv_ref[...]
    ckv_ref[...] = ckv
    ckvb = ckv.astype(BF16)
    ckvb_ref[...] = ckvb
    if with_kcat:
        kcat_ref = rest[0]
        placed = pltpu.roll(roped, MLA_NOPE, 1)
        kn = _dot(ckvb, wuk_ref[...])
        for h in range(MLA_HEADS):
            kcat_ref[:, h * HEAD_PAD:(h + 1) * HEAD_PAD] = (
                kn[:, h * HEAD_PAD:(h + 1) * HEAD_PAD] + placed).astype(BF16)

    qmb_ref[...] = proj(_C_QMB)
    kmb_ref[...] = proj(_C_KMB)
    vmb_ref[...] = proj(_C_VMB)
    qmem_ref[...] = proj(_C_QMEM) * MEM_QSCALE


def _in_proj_weights(w_in, w_uq, w_ukv):
    half = MLA_ROPE // 2
    kr = w_in[:, 640:672]
    kr_sw = jnp.concatenate([kr[:, half:], kr[:, :half]], axis=1)
    wa = jnp.concatenate(
        [w_in[:, 0:640], w_in[:, 672:2720], kr, kr_sw, jnp.zeros((D_MODEL, LANES - 2 * MLA_ROPE), F32)], axis=1)
    wq = w_uq.reshape(MLA_Q_RANK, MLA_HEADS, MLA_NOPE + MLA_ROPE)
    rope = wq[..., MLA_NOPE:]
    rope_sw = jnp.concatenate([rope[..., half:], rope[..., :half]], axis=-1)
    wuq = jnp.concatenate([wq, rope_sw], axis=-1).reshape(MLA_Q_RANK, MLA_HEADS * HEAD_PAD)
    wuk = jnp.concatenate(
        [w_ukv[..., :MLA_NOPE], jnp.zeros((MLA_KV_RANK, MLA_HEADS, HEAD_PAD - MLA_NOPE), F32)], axis=-1
    ).reshape(MLA_KV_RANK, MLA_HEADS * HEAD_PAD)
    return wa.astype(BF16), wuq.astype(BF16), wuk.astype(BF16)


def _in_proj(x2d, g_attn, wa, g_q_a, wuq, g_kv_a, wuk, *, tm, pos_base, period, with_kcat):
    r = x2d.shape[0]
    assert r % tm == 0 and period & (period - 1) == 0
    row = lambda c: pl.BlockSpec((tm, c), lambda i: (i, 0))
    outs = [(MLA_KV_RANK, F32), (MLA_KV_RANK, BF16), (MLA_ROPE, F32), (MOBA_W, F32), (MOBA_W, F32),
            (MOBA_W, F32), (MEM_W, F32), (MLA_HEADS * HEAD_PAD, BF16)]
    if with_kcat:
        outs.append((MLA_HEADS * HEAD_PAD, BF16))
    est = 2 * (wa.size + wuq.size + wuk.size) * 2 + 2 * tm * D_MODEL * 4 + 6 * tm * 1024 * 4
    est += 2 * sum(tm * max(c, LANES) * jnp.dtype(d).itemsize for c, d in outs)
    return pl.pallas_call(
        functools.partial(_in_proj_kernel, tm=tm, pos_base=pos_base, period=period, with_kcat=with_kcat),
        grid=(r // tm,),
        in_specs=[row(D_MODEL), _full((1, D_MODEL)), _full(wa.shape), _full((1, MLA_Q_RANK)), _full(wuq.shape),
                  _full((1, MLA_KV_RANK)), _full(wuk.shape), _full((8, LANES))],
        out_specs=[row(c) for c, _ in outs],
        out_shape=[jax.ShapeDtypeStruct((r, c), d) for c, d in outs],
        compiler_params=_params(("arbitrary",), est),
        name="in_proj",
    )(x2d, g_attn.reshape(1, -1), wa, g_q_a.reshape(1, -1), wuq, g_kv_a.reshape(1, -1), wuk, _rope_table())


def _mem_attn_kernel(q_ref, k_ref, v_ref, o_ref):
    kb = k_ref[0].astype(BF16)
    vb = v_ref[0].astype(BF16)
    for h in range(MEM_HEADS):
        sl = slice(h * MEM_HDIM, (h + 1) * MEM_HDIM)
        s = _dot_nt(q_ref[0, :, sl].astype(BF16), kb[:, sl])
        p = jnp.exp2(s - jnp.max(s, axis=-1, keepdims=True))
        l = jnp.sum(p, axis=-1, keepdims=True)
        o_ref[0, :, sl] = _dot(p.astype(BF16), vb[:, sl]) / l


def _mem_attn(q, mem_k, mem_v, *, tr):
    g, r, _ = q.shape
    m = mem_k.shape[1]
    return pl.pallas_call(
        _mem_attn_kernel,
        grid=(g, r // tr),
        in_specs=[pl.BlockSpec((1, tr, MEM_W), lambda i, j: (i, j, 0)),
                  pl.BlockSpec((1, m, MEM_W), lambda i, j: (i, 0, 0)),
                  pl.BlockSpec((1, m, MEM_W), lambda i, j: (i, 0, 0))],
        out_specs=pl.BlockSpec((1, tr, MEM_W), lambda i, j: (i, j, 0)),
        out_shape=jax.ShapeDtypeStruct((g, r, MEM_W), F32),
        compiler_params=_params(("arbitrary", "arbitrary"), 4 * tr * MEM_W * 4 + 4 * m * MEM_W * 4 + 4 * tr * m * 4),
        name="mem_attn",
    )(q, mem_k, mem_v)


def _mla_prompt_kernel(qi_ref, ki_ref, q_ref, k_ref, v_ref, wuv_ref, o_ref, m_sc, l_sc, a_sc, p_sc, acc_sc,
                       *, tq, tk):
    pidx = pl.program_id(1)
    qi = qi_ref[pidx]
    ki = ki_ref[pidx]
    last = (qi * tq + tq - 1) // tk

    @pl.when(ki == 0)
    def _():
        m_sc[...] = jnp.full(m_sc.shape, NEG, F32)
        l_sc[...] = jnp.zeros(l_sc.shape, F32)
        acc_sc[...] = jnp.zeros(acc_sc.shape, F32)

    rel = (qi * tq + lax.broadcasted_iota(jnp.int32, (tq, tk), 0)) - (
        ki * tk + lax.broadcasted_iota(jnp.int32, (tq, tk), 1))
    visible = rel >= 0
    for h in range(MLA_HEADS):
        sl = slice(h * HEAD_PAD, (h + 1) * HEAD_PAD)
        s = jnp.where(visible, _dot_nt(q_ref[0, :, sl], k_ref[0, :, sl]), NEG)
        m_old = m_sc[h]
        m_new = jnp.maximum(m_old, jnp.max(s, axis=-1, keepdims=True))
        alpha = jnp.exp2(m_old - m_new)
        p = jnp.exp2(s - m_new)
        l_sc[h] = alpha * l_sc[h] + jnp.sum(p, axis=-1, keepdims=True)
        m_sc[h] = m_new
        a_sc[h * tq:(h + 1) * tq, :] = alpha
        p_sc[h * tq:(h + 1) * tq, :] = p.astype(BF16)
    acc_sc[...] = acc_sc[...] * a_sc[...] + _dot(p_sc[...], v_ref[0])

    @pl.when(ki == last)
    def _():
        for h in range(0, MLA_HEADS, 2):
            pair = None
            for hh in (h, h + 1):
                lat = (acc_sc[hh * tq:(hh + 1) * tq, :] / l_sc[hh]).astype(BF16)
                o = _dot(lat, wuv_ref[hh])
                pair = o if pair is None else pair + o
            o_ref[0, :, h * MLA_VDIM:(h + 2) * MLA_VDIM] = pair.astype(BF16)


def _causal_pairs(t, tq, tk):
    qs, ks = [], []
    for qi in range(t // tq):
        for ki in range((qi * tq + tq - 1) // tk + 1):
            qs.append(qi)
            ks.append(ki)
    return jnp.asarray(qs, jnp.int32), jnp.asarray(ks, jnp.int32)


def _mla_prompt(qcat, kcat, ckvb, wuv, *, tq, tk):
    b, t, _ = qcat.shape
    qs, ks = _causal_pairs(t, tq, tk)
    w = MLA_HEADS * HEAD_PAD
    grid_spec = pltpu.PrefetchScalarGridSpec(
        num_scalar_prefetch=2,
        grid=(b, int(qs.shape[0])),
        in_specs=[pl.BlockSpec((1, tq, w), lambda i, p, qs, ks: (i, qs[p], 0)),
                  pl.BlockSpec((1, tk, w), lambda i, p, qs, ks: (i, ks[p], 0)),
                  pl.BlockSpec((1, tk, MLA_KV_RANK), lambda i, p, qs, ks: (i, ks[p], 0)),
                  pl.BlockSpec((MLA_HEADS, MLA_KV_RANK, 2 * MLA_VDIM), lambda i, p, qs, ks: (0, 0, 0))],
        out_specs=pl.BlockSpec((1, tq, MLA_HEADS * MLA_VDIM), lambda i, p, qs, ks: (i, qs[p], 0)),
        scratch_shapes=[pltpu.VMEM((MLA_HEADS, tq, 1), F32), pltpu.VMEM((MLA_HEADS, tq, 1), F32),
                        pltpu.VMEM((MLA_HEADS * tq, 1), F32), pltpu.VMEM((MLA_HEADS * tq, tk), BF16),
                        pltpu.VMEM((MLA_HEADS * tq, MLA_KV_RANK), F32)])
    est = (2 * (tq * w + tk * w + tk * MLA_KV_RANK) * 2 + 3 * MLA_HEADS * tq * LANES * 4
           + MLA_HEADS * tq * (tk * 2 + MLA_KV_RANK * 4) + 6 * tq * tk * 4)
    return pl.pallas_call(
        functools.partial(_mla_prompt_kernel, tq=tq, tk=tk),
        grid_spec=grid_spec,
        out_shape=jax.ShapeDtypeStruct((b, t, MLA_HEADS * MLA_VDIM), BF16),
        compiler_params=_params(("arbitrary", "arbitrary"), est),
        name="mla_prompt",
    )(qs, ks, qcat, kcat, ckvb, wuv)


_SEL0 = MOBA_HDIM
_FEAT0 = MOBA_HDIM + 32


def _moba_select_kernel(q_ref, k_ref, v_ref, qa_ref, ka_ref, va_ref, km_sc):
    c = pl.program_id(1)

    @pl.when(c == 0)
    def _():
        km_sc[...] = jnp.zeros(km_sc.shape, F32)

    blk = MOBA_BLOCK
    lane = lax.broadcasted_iota(jnp.int32, (blk, HEAD_PAD), 1)
    r_f = lax.broadcasted_iota(jnp.int32, (blk, HEAD_PAD), 0).astype(F32)
    past = (lane >= _SEL0) & (lane < _SEL0 + c)
    slopes = _alibi_slopes_log2()

    def head_slab(ref, h):
        pair = ref[0, :, (h // 2) * HEAD_PAD:(h // 2 + 1) * HEAD_PAD]
        if h % 2:
            pair = pltpu.roll(pair, MOBA_HDIM, 1)
        return jnp.where(lane < MOBA_HDIM, pair, 0.0)

    for h in range(MOBA_HEADS):
        q128 = head_slab(q_ref, h)
        k128 = head_slab(k_ref, h)
        v128 = head_slab(v_ref, h)
        gate = _dot_nt(q128, km_sc[h], precision=lax.Precision.HIGHEST)
        g = jnp.where(past, gate, -jnp.inf)
        bias = jnp.where(lane == _SEL0 + c, 0.0, NEG)
        for _ in range(MOBA_TOPK):
            mx = jnp.max(g, axis=-1, keepdims=True)
            hit = (g == mx) & (mx > -jnp.inf)
            first = jnp.min(jnp.where(hit, lane, HEAD_PAD), axis=-1, keepdims=True)
            pick = lane == first
            bias = jnp.where(pick, 0.0, bias)
            g = jnp.where(pick, -jnp.inf, g)
        s3 = _split3_bf16(slopes[h])
        qfeat = jnp.where(lane < _FEAT0 + 3, r_f,
                          jnp.where(lane == _FEAT0 + 3, s3[0],
                                    jnp.where(lane == _FEAT0 + 4, s3[1],
                                              jnp.where(lane == _FEAT0 + 5, s3[2], 0.0))))
        kfeat = jnp.where(lane == _FEAT0, -s3[0],
                          jnp.where(lane == _FEAT0 + 1, -s3[1],
                                    jnp.where(lane == _FEAT0 + 2, -s3[2],
                                              jnp.where(lane < _FEAT0 + 6, r_f, 0.0))))
        onehot = jnp.where(lane == _SEL0 + c, 1.0, 0.0)
        hs = slice(h * HEAD_PAD, (h + 1) * HEAD_PAD)
        qa_ref[0, :, hs] = jnp.where(lane < _SEL0, q128 * MOBA_QSCALE,
                                     jnp.where(lane < _FEAT0, bias, qfeat)).astype(BF16)
        ka_ref[0, :, hs] = jnp.where(lane < _SEL0, k128, jnp.where(lane < _FEAT0, onehot, kfeat)).astype(BF16)
        va_ref[0, :, hs] = jnp.where(lane == MOBA_HDIM, 1.0, v128).astype(BF16)
        km_sc[h, pl.ds(_SEL0 + c, 1), :] = jnp.mean(k128, axis=0, keepdims=True)


def _moba_select(qmb, kmb, vmb):
    b, t, _ = qmb.shape
    nb = t // MOBA_BLOCK
    assert nb <= 32
    w = MOBA_HEADS * HEAD_PAD
    blk_in = pl.BlockSpec((1, MOBA_BLOCK, MOBA_W), lambda i, c: (i, c, 0))
    blk_out = pl.BlockSpec((1, MOBA_BLOCK, w), lambda i, c: (i, c, 0))
    out = jax.ShapeDtypeStruct((b, t, w), BF16)
    return pl.pallas_call(
        _moba_select_kernel,
        grid=(b, nb),
        in_specs=[blk_in] * 3,
        out_specs=[blk_out] * 3,
        out_shape=[out] * 3,
        scratch_shapes=[pltpu.VMEM((MOBA_HEADS, HEAD_PAD, HEAD_PAD), F32)],
        compiler_params=_params(("arbitrary", "arbitrary"), 24 << 20),
        name="moba_select",
    )(qmb, kmb, vmb)


def _moba_prompt_kernel(c_ref, j_ref, q_ref, k_ref, v_ref, o_ref, m_sc, acc_sc):
    pidx = pl.program_id(1)
    c = c_ref[pidx]
    j = j_ref[pidx]
    blk = MOBA_BLOCK

    @pl.when(j == 0)
    def _():
        m_sc[...] = jnp.full(m_sc.shape, NEG, F32)
        acc_sc[...] = jnp.zeros(acc_sc.shape, F32)

    rel = lax.broadcasted_iota(jnp.int32, (blk, blk), 0) - lax.broadcasted_iota(jnp.int32, (blk, blk), 1)
    visible = (rel >= 0) | (j < c)
    dist = ((c - j) * blk).astype(F32)
    slopes = _alibi_slopes_log2()
    for h in range(MOBA_HEADS):
        hs = slice(h * HEAD_PAD, (h + 1) * HEAD_PAD)
        s = _dot_nt(q_ref[0, :, hs], k_ref[0, :, hs]) - dist * slopes[h]
        s = jnp.where(visible, s, NEG)
        m_old = m_sc[h]
        m_new = jnp.maximum(m_old, jnp.max(s, axis=-1, keepdims=True))
        p = jnp.exp2(s - m_new)
        acc_sc[h] = acc_sc[h] * jnp.exp2(m_old - m_new) + _dot(p.astype(BF16), v_ref[0, :, hs])
        m_sc[h] = m_new

    @pl.when(j == c)
    def _():
        lane = lax.broadcasted_iota(jnp.int32, (blk, HEAD_PAD), 1)
        for h in range(0, MOBA_HEADS, 2):
            a0 = acc_sc[h]
            a1 = acc_sc[h + 1]
            n0 = a0 / a0[:, MOBA_HDIM:MOBA_HDIM + 1]
            n1 = pltpu.roll(a1 / a1[:, MOBA_HDIM:MOBA_HDIM + 1], MOBA_HDIM, 1)
            o_ref[0, :, h * MOBA_HDIM:(h + 2) * MOBA_HDIM] = jnp.where(lane < MOBA_HDIM, n0, n1).astype(BF16)


def _moba_prompt(qa, ka, va):
    b, t, w = qa.shape
    nb = t // MOBA_BLOCK
    cs = jnp.asarray([c for c in range(nb) for _ in range(c + 1)], jnp.int32)
    js = jnp.asarray([j for c in range(nb) for j in range(c + 1)], jnp.int32)
    grid_spec = pltpu.PrefetchScalarGridSpec(
        num_scalar_prefetch=2,
        grid=(b, int(cs.shape[0])),
        in_specs=[pl.BlockSpec((1, MOBA_BLOCK, w), lambda i, p, cs, js: (i, cs[p], 0)),
                  pl.BlockSpec((1, MOBA_BLOCK, w), lambda i, p, cs, js: (i, js[p], 0)),
                  pl.BlockSpec((1, MOBA_BLOCK, w), lambda i, p, cs, js: (i, js[p], 0))],
        out_specs=pl.BlockSpec((1, MOBA_BLOCK, MOBA_W), lambda i, p, cs, js: (i, cs[p], 0)),
        scratch_shapes=[pltpu.VMEM((MOBA_HEADS, MOBA_BLOCK, 1), F32),
                        pltpu.VMEM((MOBA_HEADS, MOBA_BLOCK, HEAD_PAD), F32)])
    return pl.pallas_call(
        _moba_prompt_kernel,
        grid_spec=grid_spec,
        out_shape=jax.ShapeDtypeStruct((b, t, MOBA_W), BF16),
        compiler_params=_params(("arbitrary", "arbitrary"), 16 << 20),
        name="moba_prompt",
    )(cs, js, qa, ka, va)


def _merge_kernel(x_ref, omla_ref, omoba_ref, omem_ref, gattn_ref, wg_ref, bg_ref, wb_ref, wo_ref, y_ref):
    x = x_ref[...]
    hb = (_rms(x) * gattn_ref[...]).astype(BF16)
    merged = None
    for br, o_ref in enumerate((omla_ref, omoba_ref, omem_ref)):
        logit = _dot(hb, wg_ref[:, br * D_MODEL:(br + 1) * D_MODEL]) + bg_ref[br:br + 1, :]
        gate = 1.0 / (1.0 + jnp.exp(-logit))
        term = gate * _dot(o_ref[...].astype(BF16), wb_ref[br])
        merged = term if merged is None else merged + term
    y_ref[...] = x + _dot(merged.astype(BF16), wo_ref[...])


def _merge(x2d, o_mla, o_moba, o_mem, g_attn, wg, b_gate, wb, wo, *, tm):
    r = x2d.shape[0]
    row = lambda c: pl.BlockSpec((tm, c), lambda i: (i, 0))
    est = 2 * (wg.size + wb.size + wo.size) * 2 + 4 * tm * D_MODEL * 4 + 6 * tm * BRANCH_W * 2 + 6 * tm * D_MODEL * 4
    return pl.pallas_call(
        _merge_kernel,
        grid=(r // tm,),
        in_specs=[row(D_MODEL), row(BRANCH_W), row(BRANCH_W), row(BRANCH_W), _full((1, D_MODEL)), _full(wg.shape),
                  _full(b_gate.shape), _full(wb.shape), _full(wo.shape)],
        out_specs=row(D_MODEL),
        out_shape=jax.ShapeDtypeStruct((r, D_MODEL), F32),
        compiler_params=_params(("arbitrary",), est),
        name="merge",
    )(x2d, o_mla, o_moba, o_mem, g_attn.reshape(1, -1), wg, b_gate, wb, wo)


_FF_CHUNK = 1024


def _mlp_kernel(x_ref, gmlp_ref, wup_ref, wdown_ref, gfin_ref, y_ref):
    x = x_ref[...]
    hb = (_rms(x) * gmlp_ref[...]).astype(BF16)
    acc = x
    for c in range(D_FF // _FF_CHUNK):
        cs = slice(c * _FF_CHUNK, (c + 1) * _FF_CHUNK)
        u = jnp.maximum(_dot(hb, wup_ref[:, cs]), 0.0)
        acc = acc + _dot((u * u).astype(BF16), wdown_ref[cs, :])
    y_ref[...] = _rms(acc) * gfin_ref[...]


def _mlp(x2d, g_mlp, wup, wdown, g_final, *, tm):
    r = x2d.shape[0]
    row = pl.BlockSpec((tm, D_MODEL), lambda i: (i, 0))
    est = 2 * (wup.size + wdown.size) * 2 + 4 * tm * D_MODEL * 4 + 4 * tm * _FF_CHUNK * 4 + 2 * tm * D_MODEL * 4
    return pl.pallas_call(
        _mlp_kernel,
        grid=(r // tm,),
        in_specs=[row, _full((1, D_MODEL)), _full(wup.shape), _full(wdown.shape), _full((1, D_MODEL))],
        out_specs=row,
        out_shape=jax.ShapeDtypeStruct((r, D_MODEL), F32),
        compiler_params=_params(("arbitrary",), est),
        name="mlp",
    )(x2d, g_mlp.reshape(1, -1), wup, wdown, g_final.reshape(1, -1))


def _mla_sample_kernel(pt_ref, qcat_ref, ckvn_ref, kropen_ref, wukt_ref, wuv_ref, *rest, pages, page, steps, s_len):
    ckv_refs = rest[:pages]
    kr_refs = rest[pages:2 * pages]
    o_ref = rest[2 * pages]
    qlat_sc, qrope_sc, m_sc, l_sc, acc_sc = rest[2 * pages + 1:]
    st = pl.program_id(1)
    rows = MLA_HEADS * s_len

    @pl.when(st == 0)
    def _():
        for h in range(MLA_HEADS):
            qh = qcat_ref[0, :, h * HEAD_PAD:(h + 1) * HEAD_PAD]
            qlat_sc[h * s_len:(h + 1) * s_len, :] = _dot(qh.astype(BF16), wukt_ref[h])
            qrope_sc[h * s_len:(h + 1) * s_len, :] = pltpu.roll(qh, HEAD_PAD - MLA_NOPE, 1)[:, :MLA_ROPE]
        m_sc[...] = jnp.full(m_sc.shape, NEG, F32)
        l_sc[...] = jnp.zeros(l_sc.shape, F32)
        acc_sc[...] = jnp.zeros(acc_sc.shape, F32)

    def update(s, vals):
        m_old = m_sc[...]
        m_new = jnp.maximum(m_old, jnp.max(s, axis=-1, keepdims=True))
        alpha = jnp.exp2(m_old - m_new)
        p = jnp.exp2(s - m_new)
        l_sc[...] = alpha * l_sc[...] + jnp.sum(p, axis=-1, keepdims=True)
        acc_sc[...] = alpha * acc_sc[...] + _dot(p.astype(BF16), vals)
        m_sc[...] = m_new

    qlat = qlat_sc[...].astype(BF16)
    qrope = qrope_sc[...].astype(BF16)
    kv = jnp.concatenate([r[0] for r in ckv_refs], axis=0).astype(BF16)
    kr = jnp.concatenate([r[0] for r in kr_refs], axis=0).astype(BF16)
    update(_dot_nt(qlat, kv) + _dot_nt(qrope, kr), kv)

    @pl.when(st == steps - 1)
    def _():
        kvn = ckvn_ref[0]
        npad = kvn.shape[0]
        s = _dot_nt(qlat, kvn) + _dot_nt(qrope, kropen_ref[0])
        qpos = lax.broadcasted_iota(jnp.int32, (rows, npad), 0) & (s_len - 1)
        kpos = lax.broadcasted_iota(jnp.int32, (rows, npad), 1)
        update(jnp.where(kpos <= qpos, s, NEG), kvn)
        lat = acc_sc[...] / l_sc[...]
        for h in range(MLA_HEADS):
            o_ref[0, h * s_len:(h + 1) * s_len, :] = _dot(
                lat[h * s_len:(h + 1) * s_len, :].astype(BF16), wuv_ref[h])


def _mla_sample(qcat, ckv_new, krope_new, cache_ckv, cache_krope, page_table, wukt, wuv, *, pages):
    n, s_len, w = qcat.shape
    n_pages = page_table.shape[1]
    page = cache_ckv.shape[1]
    assert n_pages % pages == 0 and s_len & (s_len - 1) == 0
    steps = n_pages // pages
    rows = MLA_HEADS * s_len

    def page_spec(width, g):
        return pl.BlockSpec((1, page, width), lambda i, st, pt: (pt[i * n_pages + st * pages + g], 0, 0))

    npad = ckv_new.shape[1]
    seq = lambda r, c: pl.BlockSpec((1, r, c), lambda i, st, pt: (i, 0, 0))
    const = lambda shp: pl.BlockSpec(shp, lambda i, st, pt: (0,) * len(shp))
    grid_spec = pltpu.PrefetchScalarGridSpec(
        num_scalar_prefetch=1,
        grid=(n, steps),
        in_specs=([seq(s_len, w), seq(npad, MLA_KV_RANK), seq(npad, MLA_ROPE), const(wukt.shape), const(wuv.shape)]
                  + [page_spec(MLA_KV_RANK, g) for g in range(pages)]
                  + [page_spec(MLA_ROPE, g) for g in range(pages)]),
        out_specs=seq(rows, MLA_VDIM),
        scratch_shapes=[pltpu.VMEM((rows, MLA_KV_RANK), F32), pltpu.VMEM((rows, MLA_ROPE), F32),
                        pltpu.VMEM((rows, 1), F32), pltpu.VMEM((rows, 1), F32),
                        pltpu.VMEM((rows, MLA_KV_RANK), F32)])
    est = 3 * pages * page * (MLA_KV_RANK + LANES) * 4 + (4 << 20)
    return pl.pallas_call(
        functools.partial(_mla_sample_kernel, pages=pages, page=page, steps=steps, s_len=s_len),
        grid_spec=grid_spec,
        out_shape=jax.ShapeDtypeStruct((n, rows, MLA_VDIM), F32),
        compiler_params=_params(("arbitrary", "arbitrary"), est),
        name="mla_sample",
    )(page_table.reshape(-1), qcat, ckv_new, krope_new, wukt, wuv,
      *([cache_ckv] * pages), *([cache_krope] * pages))


def _moba_sample_kernel(pt_ref, q_ref, kn_ref, vn_ref, *rest, ppb, blocks, steps, s_len, past_len):
    npg = ppb * blocks
    k_refs = rest[:npg]
    v_refs = rest[npg:2 * npg]
    o_ref = rest[2 * npg]
    qb_sc, km_sc, m_sc, l_sc, part_sc = rest[2 * npg + 1:]
    st = pl.program_id(1)
    rows = MOBA_HEADS * s_len
    blk = MOBA_BLOCK
    n_blk = steps * blocks
    slopes = _alibi_slopes_log2()

    rowi = lax.broadcasted_iota(jnp.int32, (rows, 1), 0)
    slope_col = jnp.zeros((rows, 1), F32)
    for h in range(MOBA_HEADS):
        slope_col = jnp.where((rowi >= h * s_len) & (rowi < (h + 1) * s_len), slopes[h], slope_col)
    tok_col = (rowi & (s_len - 1)).astype(F32)
    lane = lax.broadcasted_iota(jnp.int32, (rows, LANES), 1)
    row_head = lax.broadcasted_iota(jnp.int32, (rows, MOBA_W), 0) // s_len
    lane_head = lax.broadcasted_iota(jnp.int32, (rows, MOBA_W), 1) // MOBA_HDIM
    own = row_head == lane_head

    @pl.when(st == 0)
    def _():
        qb_sc[...] = (q_ref[0] * MOBA_QSCALE).astype(BF16)
        km_sc[...] = jnp.zeros(km_sc.shape, F32)
        m_sc[...] = jnp.full(m_sc.shape, NEG, F32)
        l_sc[...] = jnp.zeros(l_sc.shape, F32)

    def partial(kb, vb, pos0, causal):
        nk = kb.shape[0]
        s = _dot_nt(qb_sc[...], kb)
        kidx = lax.broadcasted_iota(jnp.int32, (rows, nk), 1).astype(F32)
        s = s - slope_col * ((past_len - pos0) + tok_col - kidx)
        if causal:
            s = jnp.where(kidx <= tok_col, s, NEG)
        m = jnp.max(s, axis=-1, keepdims=True)
        p = jnp.exp2(s - m)
        l = jnp.sum(p, axis=-1, keepdims=True)
        pv = jnp.where(own, _dot(p.astype(BF16), vb), 0.0)
        o = pv[:, 0:LANES]
        for i in range(1, MOBA_W // LANES):
            o = o + pv[:, i * LANES:(i + 1) * LANES]
        return m, l, o

    for g in range(blocks):
        kf = jnp.concatenate([k_refs[g * ppb + i][0] for i in range(ppb)], axis=0)
        vb = jnp.concatenate([v_refs[g * ppb + i][0] for i in range(ppb)], axis=0).astype(BF16)
        j = st * blocks + g
        m, l, o = partial(kf.astype(BF16), vb, (j * blk).astype(F32), False)
        km_sc[pl.ds(j, 1), :] = jnp.mean(kf, axis=0, keepdims=True)
        m_sc[...] = jnp.where(lane == j, m, m_sc[...])
        l_sc[...] = jnp.where(lane == j, l, l_sc[...])
        part_sc[j] = o

    @pl.when(st == steps - 1)
    def _():
        gate = _dot_nt(q_ref[0], km_sc[...], precision=lax.Precision.HIGHEST)
        g = jnp.where(lane < n_blk, gate, -jnp.inf)
        sel = jnp.zeros((rows, LANES), F32)
        for _ in range(MOBA_TOPK):
            mx = jnp.max(g, axis=-1, keepdims=True)
            first = jnp.min(jnp.where(g == mx, lane, LANES), axis=-1, keepdims=True)
            pick = lane == first
            sel = jnp.where(pick, 1.0, sel)
            g = jnp.where(pick, -jnp.inf, g)
        picked = sel > 0.0
        m_new, l_new, o_new = partial(kn_ref[0], vn_ref[0], float(past_len), True)
        m_blk = jnp.where(picked, m_sc[...], NEG)
        m_all = jnp.maximum(jnp.max(m_blk, axis=-1, keepdims=True), m_new)
        w = jnp.where(picked, jnp.exp2(m_blk - m_all), 0.0)
        w_new = jnp.exp2(m_new - m_all)
        l_all = jnp.sum(w * l_sc[...], axis=-1, keepdims=True) + w_new * l_new
        acc = o_new * w_new
        for jb in range(n_blk):
            acc = acc + part_sc[jb] * w[:, jb:jb + 1]
        acc = acc / l_all
        o_ref[0] = (acc + pltpu.roll(acc, MOBA_HDIM, 1))[:, :MOBA_HDIM]


def _moba_sample(qrows, k_new, v_new, cache_k, cache_v, page_table, *, s_len, blocks):
    n, rows, _ = qrows.shape
    npad = k_new.shape[1]
    n_pages = page_table.shape[1]
    page = cache_k.shape[1]
    ppb = MOBA_BLOCK // page
    n_blk = n_pages // ppb
    assert n_blk * ppb == n_pages and n_blk % blocks == 0 and MOBA_TOPK <= n_blk <= LANES
    assert s_len & (s_len - 1) == 0
    steps = n_blk // blocks
    npg = ppb * blocks
    ck = cache_k.reshape(cache_k.shape[0], page, MOBA_W)
    cv = cache_v.reshape(cache_v.shape[0], page, MOBA_W)

    def page_spec(g):
        return pl.BlockSpec((1, page, MOBA_W), lambda i, st, pt: (pt[i * n_pages + st * npg + g], 0, 0))

    seq = lambda r, c: pl.BlockSpec((1, r, c), lambda i, st, pt: (i, 0, 0))
    grid_spec = pltpu.PrefetchScalarGridSpec(
        num_scalar_prefetch=1,
        grid=(n, steps),
        in_specs=[seq(rows, MOBA_W), seq(npad, MOBA_W), seq(npad, MOBA_W)] + [page_spec(g) for g in range(npg)] * 2,
        out_specs=seq(rows, MOBA_HDIM),
        scratch_shapes=[pltpu.VMEM((rows, MOBA_W), BF16), pltpu.VMEM((LANES, MOBA_W), F32),
                        pltpu.VMEM((rows, LANES), F32), pltpu.VMEM((rows, LANES), F32),
                        pltpu.VMEM((n_blk, rows, LANES), F32)])
    est = 2 * 2 * npg * page * MOBA_W * 4 + n_blk * rows * LANES * 4 + blocks * MOBA_BLOCK * MOBA_W * 8 + (4 << 20)
    return pl.pallas_call(
        functools.partial(_moba_sample_kernel, ppb=ppb, blocks=blocks, steps=steps, s_len=s_len,
                          past_len=n_pages * page),
        grid_spec=grid_spec,
        out_shape=jax.ShapeDtypeStruct((n, rows, MOBA_HDIM), F32),
        compiler_params=_params(("arbitrary", "arbitrary"), est),
        name="moba_sample",
    )(page_table.reshape(-1), qrows, k_new, v_new, *([ck] * npg), *([cv] * npg))


def kernel(x_prompt, x_sample, mem_prompt, cache_mla_ckv, cache_mla_krope, cache_moba_k, cache_moba_v,
           cache_mem_k, cache_mem_v, page_table, g_attn, w_in, b_gate, g_q_a, w_uq, g_kv_a, w_ukv,
           g_mem, w_mem_kv, w_branch, w_o, g_mlp, w_up, w_down, g_final):
    b, t, _ = x_prompt.shape
    db, s_len, _ = x_sample.shape
    past_len = page_table.shape[1] * cache_moba_k.shape[1]

    wa, wuq, wuk = _in_proj_weights(w_in, w_uq, w_ukv)
    wukt = jnp.pad(jnp.transpose(w_ukv[..., :MLA_NOPE], (1, 2, 0)),
                   ((0, 0), (0, HEAD_PAD - MLA_NOPE), (0, 0))).astype(BF16)
    wuv = jnp.transpose(w_ukv[..., MLA_NOPE:], (1, 0, 2))
    zero = jnp.zeros_like(wuv)
    even = (jnp.arange(MLA_HEADS) % 2 == 0)[:, None, None]
    wuv_pair = jnp.concatenate([jnp.where(even, wuv, zero), jnp.where(even, zero, wuv)], axis=-1).astype(BF16)
    wuv = wuv.astype(BF16)
    wg = w_in[:, 2720:].astype(BF16)
    wb = w_branch.astype(BF16)
    wo = w_o.astype(BF16)
    wup = w_up.astype(BF16)
    wdown = w_down.astype(BF16)

    def finish(x2d, o_mla, o_moba, o_mem, tm):
        x1 = _merge(x2d, o_mla, o_moba, o_mem, g_attn, wg, b_gate, wb, wo, tm=tm)
        return _mlp(x1, g_mlp, wup, wdown, g_final, tm=tm)

    xp = x_prompt.reshape(b * t, D_MODEL)
    ckv_p, ckvb_p, krope_p, kmb_p, vmb_p, qmb_p, qmem_p, qcat_p, kcat_p = _in_proj(
        xp, g_attn, wa, g_q_a, wuq, g_kv_a, wuk, tm=512, pos_base=0, period=t, with_kcat=True)
    r3 = lambda a: a.reshape(b, t, a.shape[-1])
    o_mla = _mla_prompt(r3(qcat_p), r3(kcat_p), r3(ckvb_p), wuv_pair, tq=256, tk=512)
    qa, ka, va = _moba_select(r3(qmb_p), r3(kmb_p), r3(vmb_p))
    o_moba = _moba_prompt(qa, ka, va)
    mem_k_p, mem_v_p = _mem_kv(mem_prompt, g_mem, w_mem_kv)
    o_mem = _mem_attn(r3(qmem_p), mem_k_p, mem_v_p, tr=1024)
    flat = lambda a: a.reshape(b * t, a.shape[-1])
    y_prompt = finish(xp, flat(o_mla), flat(o_moba), flat(o_mem), 512).reshape(b, t, D_MODEL)

    xs = x_sample.reshape(db * s_len, D_MODEL)
    tm_s = math.gcd(db * s_len, 512)
    ckv_s, _, krope_s, kmb_s, vmb_s, qmb_s, qmem_s, qcat_s = _in_proj(
        xs, g_attn, wa, g_q_a, wuq, g_kv_a, wuk, tm=tm_s, pos_base=past_len, period=s_len, with_kcat=False)
    s3 = lambda a: a.reshape(db, s_len, a.shape[-1])
    new_keys = lambda a: jnp.pad(s3(a), ((0, 0), (0, 16 - s_len), (0, 0))).astype(BF16)
    from_rows = lambda a: a.reshape(db, MOBA_HEADS, s_len, a.shape[-1]).transpose(0, 2, 1, 3).reshape(
        db * s_len, MOBA_HEADS * a.shape[-1])
    o_mla_s = from_rows(_mla_sample(s3(qcat_s).astype(F32), new_keys(ckv_s), new_keys(krope_s), cache_mla_ckv,
                                    cache_mla_krope, page_table, wukt, wuv,
                                    pages=math.gcd(page_table.shape[1], 16)))
    qh = s3(qmb_s).reshape(db, s_len, MOBA_HEADS, MOBA_HDIM)
    eye = jnp.eye(MOBA_HEADS, dtype=jnp.bool_)[None, :, None, :, None]
    qrows = jnp.where(eye, qh[:, None], 0.0).reshape(db, MOBA_HEADS * s_len, MOBA_W)
    o_moba_s = from_rows(_moba_sample(qrows, new_keys(kmb_s), new_keys(vmb_s), cache_moba_k, cache_moba_v,
                                      page_table, s_len=s_len,
                                      blocks=math.gcd(past_len // MOBA_BLOCK, 4)))
    o_mem_s = _mem_attn(s3(qmem_s), cache_mem_k.reshape(db, MEM_TOKENS, MEM_W),
                        cache_mem_v.reshape(db, MEM_TOKENS, MEM_W), tr=s_len)
    fl = lambda a: a.reshape(db * s_len, a.shape[-1])
    y_sample = finish(xs, o_mla_s, o_moba_s, fl(o_mem_s), tm_s).reshape(db, s_len, D_MODEL)

    heads = lambda a, n, s: a.reshape(n, s, MOBA_HEADS, MOBA_HDIM)
    return (y_prompt, y_sample,
            ckv_p.reshape(b, t, MLA_KV_RANK), krope_p.reshape(b, t, MLA_ROPE),
            heads(kmb_p, b, t), heads(vmb_p, b, t),
            mem_k_p.reshape(b, MEM_TOKENS, MEM_HEADS, MEM_HDIM), mem_v_p.reshape(b, MEM_TOKENS, MEM_HEADS, MEM_HDIM),
            ckv_s.reshape(db, s_len, MLA_KV_RANK), krope_s.reshape(db, s_len, MLA_ROPE),
            heads(kmb_s, db, s_len), heads(vmb_s, db, s_len))
```
